```python
import math
import jax
import jax.numpy as jnp
from jax import lax
import numpy as np

D_MODEL = 1024
BATCH = 4
SEQ = 4096
DEPTH = 4
DEC_BATCH = 8
DEC_SEQ = 32
PAST_LEN = 2048

CHUNK = 64
EPS = 1e-6
GDN_HEADS = 4
GDN_DK = 128
GDN_DV = 128
GDN_CONV = 4
GDN_QK = GDN_HEADS * GDN_DK
GDN_V = GDN_HEADS * GDN_DV
GDN_QKV = 2 * GDN_QK + GDN_V
SC_WIDTH = 512
SC_CONV = 3
RG_WIDTH = 512
RG_BLOCKS = 8
RG_BLOCK_DIM = RG_WIDTH // RG_BLOCKS
RG_CONV = 4
RG_C = 8.0
N_EXPERTS = 32
TOP_K = 4
D_EXPERT = D_MODEL
SWIGLU_ALPHA = 1.702
SWIGLU_LIMIT = 7.0
MOE_BLOCK = 128
N_MOD = 6
IN_SPLITS = (GDN_QKV, GDN_V, GDN_HEADS, GDN_HEADS, SC_WIDTH, SC_WIDTH, SC_WIDTH, RG_WIDTH, RG_WIDTH, D_MODEL, D_MODEL, D_MODEL)
N_IN = sum(IN_SPLITS)

kernel_name = "hybrid_stream_gdn_shortconv_rglru_moe"


def split_points():
    return [int(v) for v in np.cumsum(IN_SPLITS)[:-1]]


def rms_norm(x, g):
    xf = x.astype(jnp.float32)
    y = xf * lax.rsqrt(jnp.mean(xf * xf, axis=-1, keepdims=True) + EPS)
    return (y * g.astype(jnp.float32)).astype(x.dtype)


def l2_norm(x):
    xf = x.astype(jnp.float32)
    return xf * lax.rsqrt(jnp.sum(xf * xf, axis=-1, keepdims=True) + EPS)


def causal_dwconv(x, w, state):
    width = w.shape[0]
    t = x.shape[1]
    xx = jnp.concatenate([state.astype(x.dtype), x], axis=1)
    y = xx[:, 0:t] * w[0]
    for j in range(1, width):
        y = y + xx[:, j:j + t] * w[j]
    return y, xx[:, t:]


def gated_delta_chunked(q, k, v, g, beta, s0, chunk):
    f32 = jnp.float32
    bsz, t, nh, dk = q.shape
    dv = v.shape[-1]
    n = t // chunk

    def blocks(a):
        a = a.astype(f32).reshape((bsz, n, chunk) + a.shape[2:])
        return jnp.moveaxis(a, (1, 3), (0, 2))

    qc = blocks(q) * (dk ** -0.5)
    kc = blocks(k)
    vc = blocks(v)
    bc = blocks(beta)
    gc = jnp.cumsum(blocks(g), axis=-1)
    tril = jnp.tril(jnp.ones((chunk, chunk), bool))
    strict = jnp.tril(jnp.ones((chunk, chunk), bool), -1)
    diff = gc[..., :, None] - gc[..., None, :]
    decay = jnp.where(tril, jnp.exp(jnp.where(tril, diff, 0.0)), 0.0)
    kb = kc * bc[..., None]
    a_mat = jnp.where(strict, jnp.einsum('nbhid,nbhjd->nbhij', kb, kc) * decay, 0.0)
    l_mat = a_mat + jnp.eye(chunk, dtype=f32)
    rhs = jnp.concatenate([vc * bc[..., None], kb * jnp.exp(gc)[..., None]], axis=-1)
    sol = lax.linalg.triangular_solve(l_mat, rhs, left_side=True, lower=True)
    u, w = sol[..., :dv], sol[..., dv:]
    qk = jnp.where(tril, jnp.einsum('nbhid,nbhjd->nbhij', qc, kc) * decay, 0.0)

    def step(s, inp):
        q_i, k_i, u_i, w_i, qk_i, gc_i = inp
        v_new = u_i - jnp.einsum('bhck,bhkv->bhcv', w_i, s)
        o_i = (jnp.einsum('bhck,bhkv->bhcv', q_i * jnp.exp(gc_i)[..., None], s)
               + jnp.einsum('bhcs,bhsv->bhcv', qk_i, v_new))
        g_last = gc_i[..., -1]
        s = (s * jnp.exp(g_last)[..., None, None]
             + jnp.einsum('bhck,bhcv->bhkv', k_i * jnp.exp(g_last[..., None] - gc_i)[..., None], v_new))
        return s, o_i

    s_fin, o = lax.scan(step, s0.astype(f32), (qc, kc, u, w, qk, gc))
    o = jnp.moveaxis(o, (0, 2), (1, 3)).reshape(bsz, t, nh, dv)
    return o, s_fin


def gdn_mixer(qkv, z, b_in, a_in, conv_w, a_log, dt_bias, norm_g, conv_state, s0):
    bsz, t, _ = qkv.shape
    qkv, new_conv = causal_dwconv(qkv, conv_w, conv_state)
    qkv = jax.nn.silu(qkv)
    q = l2_norm(qkv[..., :GDN_QK].reshape(bsz, t, GDN_HEADS, GDN_DK))
    k = l2_norm(qkv[..., GDN_QK:2 * GDN_QK].reshape(bsz, t, GDN_HEADS, GDN_DK))
    v = qkv[..., 2 * GDN_QK:].reshape(bsz, t, GDN_HEADS, GDN_DV)
    beta = jax.nn.sigmoid(b_in.astype(jnp.float32))
    g = -jnp.exp(a_log.astype(jnp.float32)) * jax.nn.softplus(a_in.astype(jnp.float32) + dt_bias.astype(jnp.float32))
    o, s_fin = gated_delta_chunked(q, k, v, g, beta, s0, min(CHUNK, t))
    zf = z.astype(jnp.float32).reshape(bsz, t, GDN_HEADS, GDN_DV)
    o = rms_norm(o, norm_g) * jax.nn.silu(zf)
    return o.reshape(bsz, t, GDN_V).astype(qkv.dtype), new_conv, s_fin.astype(s0.dtype)


def short_conv_mixer(b_gate, c_gate, x_in, conv_w, conv_state):
    y, new_conv = causal_dwconv(c_gate * x_in, conv_w, conv_state)
    return b_gate * y, new_conv


def rglru_mixer(xr, yr, conv_w, conv_b, w_r, b_r, w_i, b_i, lam, conv_state, h0):
    f32 = jnp.float32
    bsz, t, _ = xr.shape
    xc, new_conv = causal_dwconv(xr, conv_w, conv_state)
    xc = xc + conv_b
    xb = xc.reshape(bsz, t, RG_BLOCKS, RG_BLOCK_DIM)
    r = jax.nn.sigmoid(jnp.einsum('bthi,hij->bthj', xb, w_r).reshape(bsz, t, RG_WIDTH).astype(f32) + b_r.astype(f32))
    i = jax.nn.sigmoid(jnp.einsum('bthi,hij->bthj', xb, w_i).reshape(bsz, t, RG_WIDTH).astype(f32) + b_i.astype(f32))
    log_a = -RG_C * r * jax.nn.softplus(-lam.astype(f32))
    a = jnp.exp(log_a)
    bterm = jnp.sqrt(-jnp.expm1(2.0 * log_a)) * (i * xc.astype(f32))
    bterm = bterm.at[:, 0].add(a[:, 0] * h0.astype(f32))

    def combine(left, right):
        return (left[0] * right[0], right[0] * left[1] + right[1])

    _, h = lax.associative_scan(combine, (a, bterm), axis=1)
    out = h * jax.nn.gelu(yr.astype(f32))
    return out.astype(xr.dtype), new_conv, h[:, -1].astype(h0.dtype)


def moe_ffn(x2, router_w, router_b, w_gu, b_gu, w_down, b_down):
    f32 = jnp.float32
    n_tok, d = x2.shape
    nk = n_tok * TOP_K
    logits = x2.astype(f32) @ router_w.astype(f32) + router_b.astype(f32)
    top_val, top_idx = lax.top_k(logits, TOP_K)
    gates = jax.nn.softmax(top_val, axis=-1)
    flat_e = top_idx.reshape(-1)
    flat_tok = jnp.arange(nk, dtype=jnp.int32) // TOP_K
    order = jnp.argsort(flat_e)
    se = flat_e[order]
    counts = jnp.bincount(flat_e, length=N_EXPERTS)
    padded = (counts + MOE_BLOCK - 1) // MOE_BLOCK * MOE_BLOCK
    pad_end = jnp.cumsum(padded)
    pad_start = pad_end - padded
    grp_start = jnp.cumsum(counts) - counts
    dest = pad_start[se] + jnp.arange(nk) - grp_start[se]
    n_blocks = -(-(nk + N_EXPERTS * (MOE_BLOCK - 1)) // MOE_BLOCK)
    n_rows = n_blocks * MOE_BLOCK
    row_tok = jnp.full((n_rows,), n_tok, jnp.int32).at[dest].set(flat_tok[order])
    row_gate = jnp.zeros((n_rows,), f32).at[dest].set(gates.reshape(-1)[order])
    block_e = jnp.minimum(jnp.searchsorted(pad_end, jnp.arange(n_blocks) * MOE_BLOCK, side='right'), N_EXPERTS - 1)
    x_pad = jnp.concatenate([x2, jnp.zeros((1, d), x2.dtype)], axis=0)
    x_rows = x_pad[row_tok].reshape(n_blocks, MOE_BLOCK, d)

    def expert_block(args):
        xb, e = args
        gu = xb @ w_gu[e] + b_gu[e]
        gate, up = gu[:, :D_EXPERT], gu[:, D_EXPERT:]
        gate = jnp.minimum(gate, SWIGLU_LIMIT)
        up = jnp.clip(up, -SWIGLU_LIMIT, SWIGLU_LIMIT)
        act = (up + 1.0) * gate * jax.nn.sigmoid(gate * SWIGLU_ALPHA)
        return act @ w_down[e] + b_down[e]

    y_rows = lax.map(expert_block, (x_rows, block_e)).reshape(n_rows, d)
    y = jnp.zeros((n_tok + 1, d), f32).at[row_tok].add(y_rows.astype(f32) * row_gate[:, None])
    return y[:n_tok].astype(x2.dtype)


def trunk(x, c, gdn_conv0, gdn0, sconv0, rg_conv0, rg0, p):
    bsz, t, d = x.shape
    c_act = jax.nn.silu(c)
    n_gc, n_g, n_sc, n_rc, n_r = [], [], [], [], []
    for l in range(DEPTH):
        mod = c_act @ p['w_ada'][l] + p['b_ada'][l]
        sh1, sc1, g1, sh2, sc2, g2 = jnp.split(mod[:, None, :], N_MOD, axis=-1)
        h = rms_norm(x, p['norm1_g'][l]) * (1.0 + sc1) + sh1
        proj = h @ p['w_in'][l]
        (qkv, z, b_in, a_in, bg, cg, xs, xr, yr, m_a, m_b, m_c) = jnp.split(proj, split_points(), axis=-1)
        o_a, gc_new, g_new = gdn_mixer(qkv, z, b_in, a_in, p['gdn_conv_w'][l], p['gdn_a_log'][l],
                                       p['gdn_dt_bias'][l], p['gdn_norm_g'][l], gdn_conv0[l], gdn0[l])
        o_b, sc_new = short_conv_mixer(bg, cg, xs, p['sconv_w'][l], sconv0[l])
        o_c, rc_new, r_new = rglru_mixer(xr, yr, p['rg_conv_w'][l], p['rg_conv_b'][l], p['rg_w_r'][l],
                                         p['rg_b_r'][l], p['rg_w_i'][l], p['rg_b_i'][l], p['rg_lambda'][l],
                                         rg_conv0[l], rg0[l])
        merged = (jax.nn.sigmoid(m_a) * (o_a @ p['w_out_a'][l])
                  + jax.nn.sigmoid(m_b) * (o_b @ p['w_out_b'][l])
                  + jax.nn.sigmoid(m_c) * (o_c @ p['w_out_c'][l]))
        x = x + g1 * (merged @ p['w_o'][l])
        h = rms_norm(x, p['norm2_g'][l]) * (1.0 + sc2) + sh2
        ff = moe_ffn(h.reshape(bsz * t, d), p['router_w'][l], p['router_b'][l], p['moe_w_gu'][l],
                     p['moe_b_gu'][l], p['moe_w_down'][l], p['moe_b_down'][l])
        x = x + g2 * ff.reshape(bsz, t, d)
        n_gc.append(gc_new)
        n_g.append(g_new)
        n_sc.append(sc_new)
        n_rc.append(rc_new)
        n_r.append(r_new)
    y = rms_norm(x, p['final_g'])
    return y, jnp.stack(n_gc), jnp.stack(n_g), jnp.stack(n_sc), jnp.stack(n_rc), jnp.stack(n_r)


def setup_inputs(seed: int = 0) -> dict:
    key = jax.random.key(seed)
    keys = jax.random.split(key, 48)
    ctr = iter(range(48))
    f32 = jnp.float32
    D = D_MODEL

    def nrm(shape, scale):
        return jax.random.normal(keys[next(ctr)], shape, f32) * scale

    def unif(shape, lo, hi):
        return jax.random.uniform(keys[next(ctr)], shape, f32, lo, hi)

    dt = jnp.exp(unif((DEPTH, GDN_HEADS), math.log(1e-3), math.log(0.1)))
    a_lru = unif((DEPTH, RG_WIDTH), 0.9, 0.999)
    return {
        'x_prompt': nrm((BATCH, SEQ, D), 1.0),
        'x_sample': nrm((DEC_BATCH, DEC_SEQ, D), 1.0),
        'c_prompt': nrm((BATCH, D), 1.0),
        'c_sample': nrm((DEC_BATCH, D), 1.0),
        'state_gdn_conv': nrm((DEPTH, DEC_BATCH, GDN_CONV - 1, GDN_QKV), 1.0),
        'state_gdn': nrm((DEPTH, DEC_BATCH, GDN_HEADS, GDN_DK, GDN_DV), 0.05),
        'state_sconv': nrm((DEPTH, DEC_BATCH, SC_CONV - 1, SC_WIDTH), 1.0),
        'state_rglru_conv': nrm((DEPTH, DEC_BATCH, RG_CONV - 1, RG_WIDTH), 1.0),
        'state_rglru': nrm((DEPTH, DEC_BATCH, RG_WIDTH), 0.5),
        'norm1_g': 1.0 + nrm((DEPTH, D), 0.02),
        'norm2_g': 1.0 + nrm((DEPTH, D), 0.02),
        'final_g': 1.0 + nrm((D,), 0.02),
        'w_ada': nrm((DEPTH, D, N_MOD * D), 0.5 * D ** -0.5),
        'b_ada': nrm((DEPTH, N_MOD * D), 0.02),
        'w_in': nrm((DEPTH, D, N_IN), D ** -0.5),
        'gdn_conv_w': nrm((DEPTH, GDN_CONV, GDN_QKV), GDN_CONV ** -0.5),
        'gdn_a_log': jnp.log(unif((DEPTH, GDN_HEADS), 1.0, 16.0)),
        'gdn_dt_bias': dt + jnp.log(-jnp.expm1(-dt)),
        'gdn_norm_g': 1.0 + nrm((DEPTH, GDN_DV), 0.02),
        'w_out_a': nrm((DEPTH, GDN_V, D), GDN_V ** -0.5),
        'sconv_w': nrm((DEPTH, SC_CONV, SC_WIDTH), SC_CONV ** -0.5),
        'w_out_b': nrm((DEPTH, SC_WIDTH, D), SC_WIDTH ** -0.5),
        'rg_conv_w': nrm((DEPTH, RG_CONV, RG_WIDTH), RG_CONV ** -0.5),
        'rg_conv_b': nrm((DEPTH, RG_WIDTH), 0.02),
        'rg_w_r': nrm((DEPTH, RG_BLOCKS, RG_BLOCK_DIM, RG_BLOCK_DIM), RG_BLOCK_DIM ** -0.5),
        'rg_b_r': nrm((DEPTH, RG_WIDTH), 0.1),
        'rg_w_i': nrm((DEPTH, RG_BLOCKS, RG_BLOCK_DIM, RG_BLOCK_DIM), RG_BLOCK_DIM ** -0.5),
        'rg_b_i': nrm((DEPTH, RG_WIDTH), 0.1),
        'rg_lambda': jnp.log(a_lru) - jnp.log1p(-a_lru),
        'w_out_c': nrm((DEPTH, RG_WIDTH, D), RG_WIDTH ** -0.5),
        'w_o': nrm((DEPTH, D, D), D ** -0.5),
        'router_w': nrm((DEPTH, D, N_EXPERTS), D ** -0.5),
        'router_b': nrm((DEPTH, N_EXPERTS), 0.01),
        'moe_w_gu': nrm((DEPTH, N_EXPERTS, D, 2 * D_EXPERT), D ** -0.5),
        'moe_b_gu': nrm((DEPTH, N_EXPERTS, 2 * D_EXPERT), 0.02),
        'moe_w_down': nrm((DEPTH, N_EXPERTS, D_EXPERT, D), D_EXPERT ** -0.5),
        'moe_b_down': nrm((DEPTH, N_EXPERTS, D), 0.02),
    }


def reference(x_prompt, x_sample, c_prompt, c_sample, state_gdn_conv, state_gdn, state_sconv,
              state_rglru_conv, state_rglru, norm1_g, norm2_g, final_g, w_ada, b_ada, w_in,
              gdn_conv_w, gdn_a_log, gdn_dt_bias, gdn_norm_g, w_out_a, sconv_w, w_out_b,
              rg_conv_w, rg_conv_b, rg_w_r, rg_b_r, rg_w_i, rg_b_i, rg_lambda, w_out_c, w_o,
              router_w, router_b, moe_w_gu, moe_b_gu, moe_w_down, moe_b_down):
    params = {
        'norm1_g': norm1_g, 'norm2_g': norm2_g, 'final_g': final_g, 'w_ada': w_ada, 'b_ada': b_ada,
        'w_in': w_in, 'gdn_conv_w': gdn_conv_w, 'gdn_a_log': gdn_a_log, 'gdn_dt_bias': gdn_dt_bias,
        'gdn_norm_g': gdn_norm_g, 'w_out_a': w_out_a, 'sconv_w': sconv_w, 'w_out_b': w_out_b,
        'rg_conv_w': rg_conv_w, 'rg_conv_b': rg_conv_b, 'rg_w_r': rg_w_r, 'rg_b_r': rg_b_r,
        'rg_w_i': rg_w_i, 'rg_b_i': rg_b_i, 'rg_lambda': rg_lambda, 'w_out_c': w_out_c, 'w_o': w_o,
        'router_w': router_w, 'router_b': router_b, 'moe_w_gu': moe_w_gu, 'moe_b_gu': moe_b_gu,
        'moe_w_down': moe_w_down, 'moe_b_down': moe_b_down,
    }
    bp = x_prompt.shape[0]
    dt = x_prompt.dtype
    z_gdn_conv = jnp.zeros((DEPTH, bp, GDN_CONV - 1, GDN_QKV), dt)
    z_gdn = jnp.zeros((DEPTH, bp, GDN_HEADS, GDN_DK, GDN_DV), dt)
    z_sconv = jnp.zeros((DEPTH, bp, SC_CONV - 1, SC_WIDTH), dt)
    z_rg_conv = jnp.zeros((DEPTH, bp, RG_CONV - 1, RG_WIDTH), dt)
    z_rg = jnp.zeros((DEPTH, bp, RG_WIDTH), dt)
    y_prompt, p_gdn_conv, p_gdn, p_sconv, p_rg_conv, p_rg = trunk(
        x_prompt, c_prompt, z_gdn_conv, z_gdn, z_sconv, z_rg_conv, z_rg, params)
    y_sample, s_gdn_conv, s_gdn, s_sconv, s_rg_conv, s_rg = trunk(
        x_sample, c_sample, state_gdn_conv, state_gdn, state_sconv, state_rglru_conv, state_rglru, params)
    return (y_prompt, y_sample, p_gdn_conv, p_gdn, p_sconv, p_rg_conv, p_rg,
            s_gdn_conv, s_gdn, s_sconv, s_rg_conv, s_rg)
```

```python
import functools
import math

import jax
import jax.numpy as jnp
from jax import lax
from jax.experimental import pallas as pl
from jax.experimental.pallas import tpu as pltpu

F32 = jnp.float32
BF16 = jnp.bfloat16

D = 1024
DEPTH = 4
EPS = 1e-6
N_MOD = 6
GH = 4
GDK = 128
GDV = 128
GQK = GH * GDK
GV = GH * GDV
GQKV = 2 * GQK + GV
GCONV = 4
GCHUNK = 64
SCW = 512
SCK = 3
RGW = 512
RGK = 4
RG_BLOCKS = 8
RG_C = 8.0
NE = 32
TOPK = 4
DE = 1024
SW_ALPHA = 1.702
SW_LIMIT = 7.0

LANES = 128
SUBLANES = 8
VMEM_LIMIT = 56 * 1024 * 1024

NMAIN = 7680
COL_MA, COL_MB, COL_MC = 2, 3, 4
COL_BG, COL_CG, COL_XS, COL_XR, COL_YR = 10, 11, 12, 13, 14


def _cparams(sem):
    return pltpu.CompilerParams(dimension_semantics=sem, vmem_limit_bytes=VMEM_LIMIT)


def _dot(a, b):
    return jnp.dot(a.astype(BF16), b.astype(BF16), preferred_element_type=F32)


def _dot_nt(a, b):
    return lax.dot_general(a.astype(BF16), b.astype(BF16), (((1,), (1,)), ((), ())),
                           preferred_element_type=F32)


def _dot_tn(a, b):
    return lax.dot_general(a.astype(BF16), b.astype(BF16), (((0,), (0,)), ((), ())),
                           preferred_element_type=F32)


def _split3(x):
    x1 = x.astype(BF16)
    r1 = x - x1.astype(F32)
    x2 = r1.astype(BF16)
    r2 = r1 - x2.astype(F32)
    x3 = r2.astype(BF16)
    return x1, x2, x3


def _softplus(x):
    return jnp.maximum(x, 0.0) + jnp.log1p(jnp.exp(-jnp.abs(x)))


def _expm1(x):
    u = jnp.exp(x)
    um1 = u - 1.0
    y = um1 * x / jnp.log(u)
    y = jnp.where(u == 1.0, x, y)
    return jnp.where(um1 == -1.0, -1.0, y)


def _sigmoid(x):
    return 1.0 / (1.0 + jnp.exp(-x))


def _silu(x):
    return x * _sigmoid(x)


def _ada_kernel(c_ref, w_ref, b_ref, o_ref):
    c = c_ref[...]
    o_ref[...] = _dot(_silu(c), w_ref[...]) + b_ref[...]


def _ada_mods(c_all, w_ada, b_ada):
    r = c_all.shape[0]
    tn = 1536
    return pl.pallas_call(
        _ada_kernel,
        grid=(DEPTH, N_MOD * D // tn),
        in_specs=[
            pl.BlockSpec((r, D), lambda l, j: (0, 0)),
            pl.BlockSpec((None, D, tn), lambda l, j: (l, 0, j)),
            pl.BlockSpec((None, 1, tn), lambda l, j: (l, 0, j)),
        ],
        out_specs=pl.BlockSpec((None, r, tn), lambda l, j: (l, 0, j)),
        out_shape=jax.ShapeDtypeStruct((DEPTH, r, N_MOD * D), F32),
        compiler_params=_cparams(("arbitrary", "arbitrary")),
        name="ada_mods",
    )(c_all, w_ada, b_ada.reshape(DEPTH, 1, N_MOD * D))


def _inproj_kernel(x_ref, mod_ref, g_ref, w_ref, wba_ref, proj_ref, ba_ref, h_ref, *, bb, tt):
    j = pl.program_id(2)

    @pl.when(j == 0)
    def _():
        x = x_ref[...]
        var = jnp.mean(x * x, axis=-1, keepdims=True)
        y = x * lax.rsqrt(var + EPS) * g_ref[...]
        h = y * (1.0 + mod_ref[:, 1:2, :]) + mod_ref[:, 0:1, :]
        h2 = h.reshape(bb * tt, D).astype(BF16)
        h_ref[...] = h2
        ba_ref[...] = jnp.dot(h2, wba_ref[...], preferred_element_type=F32).reshape(bb, tt, LANES)

    proj_ref[...] = jnp.dot(h_ref[...], w_ref[...], preferred_element_type=F32).reshape(bb, tt, -1)


def _in_proj(x, mod, n1g, w_main, w_ba, *, bb, tt, tn=1536):
    b, t, _ = x.shape
    return pl.pallas_call(
        functools.partial(_inproj_kernel, bb=bb, tt=tt),
        grid=(b // bb, t // tt, NMAIN // tn),
        in_specs=[
            pl.BlockSpec((bb, tt, D), lambda i, s, j: (i, s, 0)),
            pl.BlockSpec((bb, N_MOD, D), lambda i, s, j: (i, 0, 0)),
            pl.BlockSpec((1, D), lambda i, s, j: (0, 0)),
            pl.BlockSpec((D, tn), lambda i, s, j: (0, j)),
            pl.BlockSpec((D, LANES), lambda i, s, j: (0, 0)),
        ],
        out_specs=[
            pl.BlockSpec((bb, tt, tn), lambda i, s, j: (i, s, j)),
            pl.BlockSpec((bb, tt, LANES), lambda i, s, j: (i, s, 0)),
        ],
        out_shape=[
            jax.ShapeDtypeStruct((b, t, NMAIN), F32),
            jax.ShapeDtypeStruct((b, t, LANES), F32),
        ],
        scratch_shapes=[pltpu.VMEM((bb * tt, D), BF16)],
        compiler_params=_cparams(("arbitrary", "arbitrary", "arbitrary")),
        name="in_proj",
    )(x, mod, n1g, w_main, w_ba)


def _causal_conv(xx_ref, x, w_ref, state0_ref, staten_ref, first, *, k, tc):
    lo = SUBLANES - (k - 1)

    @pl.when(first)
    def _():
        xx_ref[lo:SUBLANES, :] = state0_ref[...]

    @pl.when(jnp.logical_not(first))
    def _():
        xx_ref[lo:SUBLANES, :] = xx_ref[tc + lo:tc + SUBLANES, :]

    xx_ref[SUBLANES:SUBLANES + tc, :] = x
    y = xx_ref[lo:lo + tc, :] * w_ref[0:1, :]
    for j in range(1, k):
        y = y + xx_ref[lo + j:lo + j + tc, :] * w_ref[j:j + 1, :]
    staten_ref[...] = xx_ref[tc + lo:tc + SUBLANES, :]
    return y


def _gdn_kernel(qkvz_ref, ba_ref, conv0_ref, s0_ref, cw_ref, hp_ref, ng_ref,
                o_ref, convn_ref, sn_ref, xx_ref, s_ref, *, tc, ch):
    first = pl.program_id(1) == 0

    @pl.when(first)
    def _():
        s_ref[...] = s0_ref[...]

    y = _causal_conv(xx_ref, qkvz_ref[:, :GQKV], cw_ref, conv0_ref, convn_ref, first, k=GCONV, tc=tc)
    y = _silu(y)
    ba = ba_ref[...]
    beta_all = _sigmoid(ba)
    g_all = -jnp.exp(hp_ref[0:1, :]) * _softplus(ba + hp_ref[1:2, :])

    row = lax.broadcasted_iota(jnp.int32, (ch, ch), 0)
    col = lax.broadcasted_iota(jnp.int32, (ch, ch), 1)
    tril = row >= col
    strict = row > col
    tril_bf = tril.astype(BF16)
    eye = (row == col).astype(F32)
    n_lvl = int(math.log2(ch))
    lvl_masks = [jnp.logical_and(((row >> m) & 1) == 1, (col >> m) == (row >> m) - 1) for m in range(n_lvl)]
    sel_r = lax.broadcasted_iota(jnp.int32, (SUBLANES, LANES), 0)
    sel_c = lax.broadcasted_iota(jnp.int32, (SUBLANES, LANES), 1)
    sel8 = (sel_c == sel_r + GH).astype(BF16)
    scale = GDK ** -0.5
    ng = ng_ref[...]

    qn, kn = [], []
    for h in range(GH):
        qh = y[:, h * GDK:(h + 1) * GDK]
        kh = y[:, GQK + h * GDK:GQK + (h + 1) * GDK]
        qn.append(qh * lax.rsqrt(jnp.sum(qh * qh, axis=-1, keepdims=True) + EPS) * scale)
        kn.append(kh * lax.rsqrt(jnp.sum(kh * kh, axis=-1, keepdims=True) + EPS))

    for c in range(tc // ch):
        r0, r1 = c * ch, (c + 1) * ch
        g_c = g_all[r0:r1, :]
        g1, g2, g3 = _split3(g_c)
        gc_all = (jnp.dot(tril_bf, g1, preferred_element_type=F32)
                  + jnp.dot(tril_bf, g2, preferred_element_type=F32)
                  + jnp.dot(tril_bf, g3, preferred_element_type=F32))
        c1, c2, c3 = _split3(gc_all)
        nt = (((1,), (1,)), ((), ()))
        gc_rows = (lax.dot_general(sel8, c1, nt, preferred_element_type=F32)
                   + lax.dot_general(sel8, c2, nt, preferred_element_type=F32)
                   + lax.dot_general(sel8, c3, nt, preferred_element_type=F32))
        for h in range(GH):
            q = qn[h][r0:r1, :]
            k = kn[h][r0:r1, :]
            v = y[r0:r1, 2 * GQK + h * GDV:2 * GQK + (h + 1) * GDV]
            beta = beta_all[r0:r1, h:h + 1]
            gc = gc_all[:, GH + h:GH + h + 1]
            gcr = gc_rows[h:h + 1, :]
            diff = gc - gcr
            decay = jnp.where(tril, jnp.exp(jnp.where(tril, diff, 0.0)), 0.0)
            kb = k * beta
            a_mat = jnp.where(strict, _dot_nt(kb, k) * decay, 0.0)
            dinv = eye - jnp.where(lvl_masks[0], a_mat, 0.0)
            for m in range(1, n_lvl):
                off = jnp.where(lvl_masks[m], a_mat, 0.0)
                dinv = dinv - _dot(_dot(dinv, off), dinv)
            egc = jnp.exp(gc)
            rhs = jnp.concatenate([v * beta, kb * egc], axis=-1)
            sol = _dot(dinv, rhs)
            u = sol[:, :GDV]
            w = sol[:, GDV:]
            qk = jnp.where(tril, _dot_nt(q, k) * decay, 0.0)
            s = s_ref[h]
            v_new = u - _dot(w, s)
            o = _dot(q * egc, s) + _dot(qk, v_new)
            g_last = gc[ch - 1:ch, :]
            s_ref[h] = s * jnp.exp(g_last) + _dot_tn(k * jnp.exp(g_last - gc), v_new)
            z = qkvz_ref[r0:r1, GQKV + h * GDV:GQKV + (h + 1) * GDV]
            on = o * lax.rsqrt(jnp.mean(o * o, axis=-1, keepdims=True) + EPS) * ng
            o_ref[r0:r1, h * GDV:(h + 1) * GDV] = on * _silu(z)

    sn_ref[...] = s_ref[...]


def _gdn(proj, ba, conv0, s0, conv_w, hp, norm_g, *, tc, ch):
    b, t, _ = proj.shape
    return pl.pallas_call(
        functools.partial(_gdn_kernel, tc=tc, ch=ch),
        grid=(b, t // tc),
        in_specs=[
            pl.BlockSpec((None, tc, GQKV + GV), lambda i, s: (i, s, 0)),
            pl.BlockSpec((None, tc, LANES), lambda i, s: (i, s, 0)),
            pl.BlockSpec((None, GCONV - 1, GQKV), lambda i, s: (i, 0, 0)),
            pl.BlockSpec((None, GH, GDK, GDV), lambda i, s: (i, 0, 0, 0)),
            pl.BlockSpec((GCONV, GQKV), lambda i, s: (0, 0)),
            pl.BlockSpec((2, LANES), lambda i, s: (0, 0)),
            pl.BlockSpec((1, GDV), lambda i, s: (0, 0)),
        ],
        out_specs=[
            pl.BlockSpec((None, tc, GV), lambda i, s: (i, s, 0)),
            pl.BlockSpec((None, GCONV - 1, GQKV), lambda i, s: (i, 0, 0)),
            pl.BlockSpec((None, GH, GDK, GDV), lambda i, s: (i, 0, 0, 0)),
        ],
        out_shape=[
            jax.ShapeDtypeStruct((b, t, GV), F32),
            jax.ShapeDtypeStruct((b, GCONV - 1, GQKV), F32),
            jax.ShapeDtypeStruct((b, GH, GDK, GDV), F32),
        ],
        scratch_shapes=[pltpu.VMEM((tc + SUBLANES, GQKV), F32), pltpu.VMEM((GH, GDK, GDV), F32)],
        compiler_params=_cparams(("arbitrary", "arbitrary")),
        name="gdn",
    )(proj, ba, conv0, s0, conv_w, hp, norm_g)


def _convrg_kernel(bg_ref, cg_ref, xs_ref, xr_ref, yr_ref, sc0_ref, rc0_ref, h0_ref,
                   scw_ref, rcw_ref, rcb_ref, wr_ref, br_ref, wi_ref, bi_ref, lam_ref,
                   ob_ref, oc_ref, scn_ref, rcn_ref, hn_ref, xs_scr, xr_scr, h_scr, *, tc):
    first = pl.program_id(1) == 0

    @pl.when(first)
    def _():
        h_scr[...] = jnp.broadcast_to(h0_ref[...], h_scr.shape)

    u = cg_ref[...] * xs_ref[...]
    yb = _causal_conv(xs_scr, u, scw_ref, sc0_ref, scn_ref, first, k=SCK, tc=tc)
    ob_ref[...] = bg_ref[...] * yb

    xc = _causal_conv(xr_scr, xr_ref[...], rcw_ref, rc0_ref, rcn_ref, first, k=RGK, tc=tc) + rcb_ref[...]
    r = _sigmoid(_dot(xc, wr_ref[...]) + br_ref[...])
    i = _sigmoid(_dot(xc, wi_ref[...]) + bi_ref[...])
    log_a = -RG_C * r * _softplus(-lam_ref[...])
    a = jnp.exp(log_a)
    bt = jnp.sqrt(-_expm1(2.0 * log_a)) * (i * xc)
    rows = lax.broadcasted_iota(jnp.int32, (tc, RGW), 0)
    s = 1
    while s < tc:
        keep = rows >= s
        a_sh = jnp.where(keep, pltpu.roll(a, s, 0), 1.0)
        b_sh = jnp.where(keep, pltpu.roll(bt, s, 0), 0.0)
        bt = a * b_sh + bt
        a = a * a_sh
        s *= 2
    h = a * h_scr[0:1, :] + bt
    h_last = h[tc - 1:tc, :]
    h_scr[...] = jnp.broadcast_to(h_last, h_scr.shape)
    hn_ref[...] = h_last
    yr = yr_ref[...]
    gelu = 0.5 * yr * (1.0 + jnp.tanh(math.sqrt(2.0 / math.pi) * (yr + 0.044715 * (yr * yr * yr))))
    oc_ref[...] = h * gelu


def _convrg(proj, sc0, rc0, h0, scw, rcw, rcb, wr, br, wi, bi, lam, *, tc):
    b, t, _ = proj.shape

    def col(cidx):
        return pl.BlockSpec((None, tc, SCW), lambda i, s, cidx=cidx: (i, s, cidx))

    def full2(shape):
        return pl.BlockSpec(shape, lambda i, s: (0, 0))

    return pl.pallas_call(
        functools.partial(_convrg_kernel, tc=tc),
        grid=(b, t // tc),
        in_specs=[
            col(COL_BG), col(COL_CG), col(COL_XS), col(COL_XR), col(COL_YR),
            pl.BlockSpec((None, SCK - 1, SCW), lambda i, s: (i, 0, 0)),
            pl.BlockSpec((None, RGK - 1, RGW), lambda i, s: (i, 0, 0)),
            pl.BlockSpec((None, 1, RGW), lambda i, s: (i, 0, 0)),
            full2((SCK, SCW)), full2((RGK, RGW)), full2((1, RGW)),
            full2((RGW, RGW)), full2((1, RGW)), full2((RGW, RGW)), full2((1, RGW)), full2((1, RGW)),
        ],
        out_specs=[
            pl.BlockSpec((None, tc, SCW), lambda i, s: (i, s, 0)),
            pl.BlockSpec((None, tc, RGW), lambda i, s: (i, s, 0)),
            pl.BlockSpec((None, SCK - 1, SCW), lambda i, s: (i, 0, 0)),
            pl.BlockSpec((None, RGK - 1, RGW), lambda i, s: (i, 0, 0)),
            pl.BlockSpec((None, 1, RGW), lambda i, s: (i, 0, 0)),
        ],
        out_shape=[
            jax.ShapeDtypeStruct((b, t, SCW), F32),
            jax.ShapeDtypeStruct((b, t, RGW), F32),
            jax.ShapeDtypeStruct((b, SCK - 1, SCW), F32),
            jax.ShapeDtypeStruct((b, RGK - 1, RGW), F32),
            jax.ShapeDtypeStruct((b, 1, RGW), F32),
        ],
        scratch_shapes=[pltpu.VMEM((tc + SUBLANES, SCW), F32), pltpu.VMEM((tc + SUBLANES, RGW), F32),
                        pltpu.VMEM((SUBLANES, RGW), F32)],
        compiler_params=_cparams(("arbitrary", "arbitrary")),
        name="convrg",
    )(proj, proj, proj, proj, proj, sc0, rc0, h0, scw, rcw, rcb, wr, br, wi, bi, lam)


def _merge_kernel(x_ref, oa_ref, ob_ref, oc_ref, ma_ref, mb_ref, mc_ref, mod_ref,
                  wa_ref, wb_ref, wc_ref, wo_ref, g2_ref, rw_ref, rb_ref,
                  x1_ref, h2_ref, lg_ref, *, bb, tt):
    n = bb * tt

    def flat(ref):
        return ref[...].reshape(n, ref.shape[-1])

    merged = (_sigmoid(flat(ma_ref)) * _dot(flat(oa_ref), wa_ref[...])
              + _sigmoid(flat(mb_ref)) * _dot(flat(ob_ref), wb_ref[...])
              + _sigmoid(flat(mc_ref)) * _dot(flat(oc_ref), wc_ref[...]))
    y = _dot(merged, wo_ref[...]).reshape(bb, tt, D)
    x1 = x_ref[...] + mod_ref[:, 2:3, :] * y
    x1_ref[...] = x1
    var = jnp.mean(x1 * x1, axis=-1, keepdims=True)
    h2 = (x1 * lax.rsqrt(var + EPS) * g2_ref[...]) * (1.0 + mod_ref[:, 4:5, :]) + mod_ref[:, 3:4, :]
    h2_ref[...] = h2
    hf = h2.reshape(n, D)
    h_hi = hf.astype(BF16)
    h_lo = (hf - h_hi.astype(F32)).astype(BF16)
    rw = rw_ref[...]
    r_hi = rw.astype(BF16)
    r_lo = (rw - r_hi.astype(F32)).astype(BF16)
    lg = (jnp.dot(h_hi, r_hi, preferred_element_type=F32)
          + (jnp.dot(h_hi, r_lo, preferred_element_type=F32) + jnp.dot(h_lo, r_hi, preferred_element_type=F32)))
    lg_ref[...] = (lg + rb_ref[...]).reshape(bb, tt, LANES)


def _merge(x, oa, ob, oc, proj, mod, wa, wb, wc, wo, n2g, rw, rb, *, bb, tt):
    b, t, _ = x.shape

    def tok(width, cidx=0):
        return pl.BlockSpec((bb, tt, width), lambda i, s, cidx=cidx: (i, s, cidx))

    def full2(shape):
        return pl.BlockSpec(shape, lambda i, s: (0, 0))

    return pl.pallas_call(
        functools.partial(_merge_kernel, bb=bb, tt=tt),
        grid=(b // bb, t // tt),
        in_specs=[
            tok(D), tok(GV), tok(SCW), tok(RGW), tok(D, COL_MA), tok(D, COL_MB), tok(D, COL_MC),
            pl.BlockSpec((bb, N_MOD, D), lambda i, s: (i, 0, 0)),
            full2((GV, D)), full2((SCW, D)), full2((RGW, D)), full2((D, D)), full2((1, D)),
            full2((D, LANES)), full2((1, LANES)),
        ],
        out_specs=[tok(D), tok(D), tok(LANES)],
        out_shape=[
            jax.ShapeDtypeStruct((b, t, D), F32),
            jax.ShapeDtypeStruct((b, t, D), F32),
            jax.ShapeDtypeStruct((b, t, LANES), F32),
        ],
        compiler_params=_cparams(("arbitrary", "arbitrary")),
        name="merge",
    )(x, oa, ob, oc, proj, proj, proj, mod, wa, wb, wc, wo, n2g, rw, rb)


ROUTE_E, ROUTE_RANK, ROUTE_GATE = 0, 4, 8


def _route_kernel(lg_ref, rec_ref, cnt_ref, run_ref, *, tm):
    @pl.when(pl.program_id(0) == 0)
    def _():
        run_ref[...] = jnp.zeros_like(run_ref)

    lane = lax.broadcasted_iota(jnp.int32, (tm, LANES), 1)
    lanef = lane.astype(F32)
    neg = jnp.float32(-jnp.inf)
    cur = jnp.where(lane < NE, lg_ref[...], neg)
    vals, hots = [], []
    for _ in range(TOPK):
        m = jnp.max(cur, axis=-1, keepdims=True)
        idx = jnp.min(jnp.where(cur == m, lanef, float(LANES)), axis=-1, keepdims=True)
        hot = lanef == idx
        cur = jnp.where(hot, neg, cur)
        vals.append(m)
        hots.append(hot)
    es = [jnp.exp(v - vals[0]) for v in vals]
    den = es[0] + es[1] + es[2] + es[3]
    cnt = hots[0].astype(F32) + hots[1].astype(F32) + hots[2].astype(F32) + hots[3].astype(F32)
    r = lax.broadcasted_iota(jnp.int32, (tm, tm), 0)
    c = lax.broadcasted_iota(jnp.int32, (tm, tm), 1)
    prefix = jnp.dot((r > c).astype(BF16), cnt.astype(BF16), preferred_element_type=F32)
    base = prefix + run_ref[0:1, :]
    rec = jnp.zeros((tm, LANES), F32)
    for k in range(TOPK):
        e_k = jnp.sum(jnp.where(hots[k], lanef, 0.0), axis=-1, keepdims=True)
        rank_k = jnp.sum(jnp.where(hots[k], base, 0.0), axis=-1, keepdims=True)
        rec = jnp.where(lane == ROUTE_E + k, e_k, rec)
        rec = jnp.where(lane == ROUTE_RANK + k, rank_k, rec)
        rec = jnp.where(lane == ROUTE_GATE + k, es[k] / den, rec)
    rec_ref[...] = rec
    run = run_ref[0:1, :] + jnp.sum(cnt, axis=0, keepdims=True)
    run_ref[...] = jnp.broadcast_to(run, run_ref.shape)
    cnt_ref[...] = jnp.broadcast_to(run, cnt_ref.shape)


def _route(logits, *, tm):
    n = logits.shape[0]
    return pl.pallas_call(
        functools.partial(_route_kernel, tm=tm),
        grid=(n // tm,),
        in_specs=[pl.BlockSpec((tm, LANES), lambda i: (i, 0))],
        out_specs=[pl.BlockSpec((tm, LANES), lambda i: (i, 0)), pl.BlockSpec((SUBLANES, LANES), lambda i: (0, 0))],
        out_shape=[jax.ShapeDtypeStruct((n, LANES), F32), jax.ShapeDtypeStruct((SUBLANES, LANES), F32)],
        scratch_shapes=[pltpu.VMEM((SUBLANES, LANES), F32)],
        compiler_params=_cparams(("arbitrary",)),
        name="route",
    )(logits)


def _dispatch_kernel(dest_ref, h2_ref, zin_ref, xr_ref, sem, *, tm):
    del zin_ref
    base = pl.program_id(0) * tm

    def copy(j):
        return pltpu.make_async_copy(h2_ref.at[pl.ds(base + j // TOPK, 1)], xr_ref.at[pl.ds(dest_ref[0, j], 1)], sem)

    def start(j, carry):
        copy(j).start()
        return carry

    def wait(j, carry):
        copy(j).wait()
        return carry

    lax.fori_loop(0, tm * TOPK, start, 0)
    lax.fori_loop(0, tm * TOPK, wait, 0)


def _dispatch(dest_tiles, h2, n_rows, *, tm):
    n = h2.shape[0]
    zeros = jnp.zeros((n_rows, D), F32)
    return pl.pallas_call(
        functools.partial(_dispatch_kernel, tm=tm),
        grid=(n // tm,),
        in_specs=[
            pl.BlockSpec((None, 1, tm * TOPK), lambda i: (i, 0, 0), memory_space=pltpu.SMEM),
            pl.BlockSpec(memory_space=pl.ANY),
            pl.BlockSpec(memory_space=pl.ANY),
        ],
        out_specs=pl.BlockSpec(memory_space=pl.ANY),
        out_shape=jax.ShapeDtypeStruct((n_rows, D), F32),
        scratch_shapes=[pltpu.SemaphoreType.DMA],
        input_output_aliases={2: 0},
        compiler_params=_cparams(("arbitrary",)),
        name="dispatch",
    )(dest_tiles, h2, zeros)


def _expert_kernel(be_ref, x_ref, wgu_ref, bgu_ref, wd_ref, bd_ref, y_ref, wgu_bf, wd_bf):
    i = pl.program_id(0)
    prev = be_ref[jnp.maximum(i - 1, 0)]
    changed = jnp.logical_or(i == 0, be_ref[i] != prev)

    @pl.when(changed)
    def _():
        wgu_bf[...] = wgu_ref[...].astype(BF16)
        wd_bf[...] = wd_ref[...].astype(BF16)

    x = x_ref[...].astype(BF16)
    gu = jnp.dot(x, wgu_bf[...], preferred_element_type=F32) + bgu_ref[...]
    gate = jnp.minimum(gu[:, :DE], SW_LIMIT)
    up = jnp.clip(gu[:, DE:], -SW_LIMIT, SW_LIMIT)
    act = (up + 1.0) * gate * _sigmoid(gate * SW_ALPHA)
    y_ref[...] = jnp.dot(act.astype(BF16), wd_bf[...], preferred_element_type=F32) + bd_ref[...]


def _experts(block_e, x_rows, wgu, bgu, wd, bd, *, bm):
    n_rows = x_rows.shape[0]
    grid_spec = pltpu.PrefetchScalarGridSpec(
        num_scalar_prefetch=1,
        grid=(n_rows // bm,),
        in_specs=[
            pl.BlockSpec((bm, D), lambda i, be: (i, 0)),
            pl.BlockSpec((None, D, 2 * DE), lambda i, be: (be[i], 0, 0)),
            pl.BlockSpec((None, 1, 2 * DE), lambda i, be: (be[i], 0, 0)),
            pl.BlockSpec((None, DE, D), lambda i, be: (be[i], 0, 0)),
            pl.BlockSpec((None, 1, D), lambda i, be: (be[i], 0, 0)),
        ],
        out_specs=pl.BlockSpec((bm, D), lambda i, be: (i, 0)),
        scratch_shapes=[pltpu.VMEM((D, 2 * DE), BF16), pltpu.VMEM((DE, D), BF16)],
    )
    return pl.pallas_call(
        _expert_kernel,
        grid_spec=grid_spec,
        out_shape=jax.ShapeDtypeStruct((n_rows, D), F32),
        compiler_params=_cparams(("arbitrary",)),
        name="experts",
    )(block_e, x_rows, wgu, bgu, wd, bd)


def _combine_kernel(dest_ref, yr_ref, rec_ref, x1_ref, mod_ref, fg_ref, x2_ref, buf, sem, *, tm, final):
    def copy(j):
        return pltpu.make_async_copy(yr_ref.at[pl.ds(dest_ref[0, j], 1)],
                                     buf.at[j % TOPK, pl.ds(j // TOPK, 1)], sem)

    def start(j, carry):
        copy(j).start()
        return carry

    def wait(j, carry):
        copy(j).wait()
        return carry

    lax.fori_loop(0, tm * TOPK, start, 0)
    lax.fori_loop(0, tm * TOPK, wait, 0)
    rec = rec_ref[...]
    ff = rec[:, ROUTE_GATE:ROUTE_GATE + 1] * buf[0]
    for k in range(1, TOPK):
        ff = ff + rec[:, ROUTE_GATE + k:ROUTE_GATE + k + 1] * buf[k]
    x2 = x1_ref[...] + mod_ref[...] * ff
    if final:
        var = jnp.mean(x2 * x2, axis=-1, keepdims=True)
        x2 = x2 * lax.rsqrt(var + EPS) * fg_ref[...]
    x2_ref[...] = x2


def _combine(dest_tiles, y_rows, rec, x1, g2_rows, final_g, *, tm, rows_per_seq, final):
    n = x1.shape[0]
    return pl.pallas_call(
        functools.partial(_combine_kernel, tm=tm, final=final),
        grid=(n // tm,),
        in_specs=[
            pl.BlockSpec((None, 1, tm * TOPK), lambda i: (i, 0, 0), memory_space=pltpu.SMEM),
            pl.BlockSpec(memory_space=pl.ANY),
            pl.BlockSpec((tm, LANES), lambda i: (i, 0)),
            pl.BlockSpec((tm, D), lambda i: (i, 0)),
            pl.BlockSpec((None, 1, D), lambda i: ((i * tm) // rows_per_seq, 0, 0)),
            pl.BlockSpec((1, D), lambda i: (0, 0)),
        ],
        out_specs=pl.BlockSpec((tm, D), lambda i: (i, 0)),
        out_shape=jax.ShapeDtypeStruct((n, D), F32),
        scratch_shapes=[pltpu.VMEM((TOPK, tm, D), F32), pltpu.SemaphoreType.DMA],
        compiler_params=_cparams(("arbitrary",)),
        name="combine",
    )(dest_tiles, y_rows, rec, x1, g2_rows, final_g)


def _moe(h2, logits, x1, g2_rows, wgu, bgu, wd, bd, final_g, *, tm, bm, rows_per_seq, final):
    n = h2.shape[0]
    rec, cnt = _route(logits, tm=tm)
    counts = cnt[0, :NE].astype(jnp.int32)
    padded = (counts + bm - 1) // bm * bm
    pad_end = jnp.cumsum(padded)
    pad_start = pad_end - padded
    e_idx = rec[:, ROUTE_E:ROUTE_E + TOPK].astype(jnp.int32)
    rank = rec[:, ROUTE_RANK:ROUTE_RANK + TOPK].astype(jnp.int32)
    dest = pad_start[e_idx] + rank
    n_blocks = -(-(n * TOPK + NE * (bm - 1)) // bm)
    n_rows = n_blocks * bm
    block_e = jnp.minimum(jnp.searchsorted(pad_end, jnp.arange(n_blocks, dtype=jnp.int32) * bm, side='right'),
                          NE - 1).astype(jnp.int32)
    dest_tiles = dest.reshape(n // tm, 1, tm * TOPK)
    x_rows = _dispatch(dest_tiles, h2, n_rows, tm=tm)
    y_rows = _experts(block_e, x_rows, wgu, bgu.reshape(NE, 1, 2 * DE), wd, bd.reshape(NE, 1, D), bm=bm)
    return _combine(dest_tiles, y_rows, rec, x1, g2_rows, final_g, tm=tm, rows_per_seq=rows_per_seq, final=final)


def _prep_layer(p, l):
    w_in = p['w_in'][l]
    o = 0
    parts = {}
    for name, width in (('qkv', GQKV), ('z', GV), ('b', GH), ('a', GH), ('bg', SCW), ('cg', SCW), ('xs', SCW),
                        ('xr', RGW), ('yr', RGW), ('ma', D), ('mb', D), ('mc', D)):
        parts[name] = w_in[:, o:o + width]
        o += width
    w_main = jnp.concatenate([parts[k] for k in ('qkv', 'z', 'ma', 'mb', 'mc', 'bg', 'cg', 'xs', 'xr', 'yr')],
                             axis=1).astype(BF16)
    w_ba = jnp.concatenate([parts['b'], parts['a'], jnp.zeros((D, LANES - 2 * GH), F32)], axis=1).astype(BF16)
    hp = jnp.zeros((2, LANES), F32)
    hp = hp.at[0, GH:2 * GH].set(p['gdn_a_log'][l]).at[1, GH:2 * GH].set(p['gdn_dt_bias'][l])
    eye = jnp.eye(RG_BLOCKS, dtype=F32)
    wr = jnp.einsum('hij,hk->hikj', p['rg_w_r'][l], eye).reshape(RGW, RGW).astype(BF16)
    wi = jnp.einsum('hij,hk->hikj', p['rg_w_i'][l], eye).reshape(RGW, RGW).astype(BF16)
    rw = jnp.concatenate([p['router_w'][l], jnp.zeros((D, LANES - NE), F32)], axis=1)
    rb = jnp.concatenate([p['router_b'][l], jnp.zeros((LANES - NE,), F32)]).reshape(1, LANES)
    return dict(
        w_main=w_main, w_ba=w_ba, hp=hp, wr=wr, wi=wi, rw=rw, rb=rb,
        n1g=p['norm1_g'][l].reshape(1, D), n2g=p['norm2_g'][l].reshape(1, D),
        gdn_cw=p['gdn_conv_w'][l], gdn_ng=p['gdn_norm_g'][l].reshape(1, GDV),
        scw=p['sconv_w'][l], rcw=p['rg_conv_w'][l], rcb=p['rg_conv_b'][l].reshape(1, RGW),
        br=p['rg_b_r'][l].reshape(1, RGW), bi=p['rg_b_i'][l].reshape(1, RGW), lam=p['rg_lambda'][l].reshape(1, RGW),
        wa=p['w_out_a'][l].astype(BF16), wb=p['w_out_b'][l].astype(BF16), wc=p['w_out_c'][l].astype(BF16),
        wo=p['w_o'][l].astype(BF16),
        wgu=p['moe_w_gu'][l], bgu=p['moe_b_gu'][l], wd=p['moe_w_down'][l], bd=p['moe_b_down'][l],
    )


def _trunk(x, mods, gdn_conv0, gdn0, sconv0, rg_conv0, rg0, layers, final_g, *, cfg):
    b, t, _ = x.shape
    n = b * t
    n_gc, n_g, n_sc, n_rc, n_r = [], [], [], [], []
    for l in range(DEPTH):
        lw = layers[l]
        mod = mods[l]
        proj, ba = _in_proj(x, mod, lw['n1g'], lw['w_main'], lw['w_ba'], bb=cfg['bb'], tt=cfg['tt'])
        o_a, gc_new, g_new = _gdn(proj, ba, gdn_conv0[l], gdn0[l], lw['gdn_cw'], lw['hp'], lw['gdn_ng'],
                                  tc=cfg['tc'], ch=cfg['ch'])
        o_b, o_c, sc_new, rc_new, r_new = _convrg(
            proj, sconv0[l], rg_conv0[l], rg0[l].reshape(b, 1, RGW), lw['scw'], lw['rcw'], lw['rcb'],
            lw['wr'], lw['br'], lw['wi'], lw['bi'], lw['lam'], tc=cfg['tc'])
        x1, h2, logits = _merge(x, o_a, o_b, o_c, proj, mod, lw['wa'], lw['wb'], lw['wc'], lw['wo'], lw['n2g'],
                                lw['rw'], lw['rb'], bb=cfg['bb'], tt=cfg['tt'])
        g2_rows = mod[:, 5:6, :]
        x2 = _moe(h2.reshape(n, D), logits.reshape(n, LANES), x1.reshape(n, D), g2_rows,
                  lw['wgu'], lw['bgu'], lw['wd'], lw['bd'], final_g,
                  tm=cfg['tm'], bm=cfg['bm'], rows_per_seq=t, final=(l == DEPTH - 1))
        x = x2.reshape(b, t, D)
        n_gc.append(gc_new)
        n_g.append(g_new)
        n_sc.append(sc_new)
        n_rc.append(rc_new)
        n_r.append(r_new.reshape(b, RGW))
    return x, jnp.stack(n_gc), jnp.stack(n_g), jnp.stack(n_sc), jnp.stack(n_rc), jnp.stack(n_r)


def kernel(x_prompt, x_sample, c_prompt, c_sample, state_gdn_conv, state_gdn, state_sconv, state_rglru_conv, state_rglru, norm1_g, norm2_g, final_g, w_ada, b_ada, w_in, gdn_conv_w, gdn_a_log, gdn_dt_bias, gdn_norm_g, w_out_a, sconv_w, w_out_b, rg_conv_w, rg_conv_b, rg_w_r, rg_b_r, rg_w_i, rg_b_i, rg_lambda, w_out_c, w_o, router_w, router_b, moe_w_gu, moe_b_gu, moe_w_down, moe_b_down):
    p = dict(norm1_g=norm1_g, norm2_g=norm2_g, w_in=w_in, gdn_conv_w=gdn_conv_w, gdn_a_log=gdn_a_log,
             gdn_dt_bias=gdn_dt_bias, gdn_norm_g=gdn_norm_g, w_out_a=w_out_a, sconv_w=sconv_w, w_out_b=w_out_b,
             rg_conv_w=rg_conv_w, rg_conv_b=rg_conv_b, rg_w_r=rg_w_r, rg_b_r=rg_b_r, rg_w_i=rg_w_i, rg_b_i=rg_b_i,
             rg_lambda=rg_lambda, w_out_c=w_out_c, w_o=w_o, router_w=router_w, router_b=router_b,
             moe_w_gu=moe_w_gu, moe_b_gu=moe_b_gu, moe_w_down=moe_w_down, moe_b_down=moe_b_down)
    bp, tp, _ = x_prompt.shape
    bs, ts, _ = x_sample.shape
    layers = [_prep_layer(p, l) for l in range(DEPTH)]
    r_pad = -(-(bp + bs) // SUBLANES) * SUBLANES
    c_all = jnp.concatenate([c_prompt, c_sample, jnp.zeros((r_pad - bp - bs, D), F32)], axis=0)
    mods = _ada_mods(c_all, w_ada, b_ada)
    mods_p = [mods[l, :bp].reshape(bp, N_MOD, D) for l in range(DEPTH)]
    mods_s = [mods[l, bp:bp + bs].reshape(bs, N_MOD, D) for l in range(DEPTH)]
    fg = final_g.reshape(1, D)
    f32 = x_prompt.dtype
    z = lambda *s: jnp.zeros(s, f32)
    cfg_p = dict(bb=1, tt=512, tc=256, ch=min(GCHUNK, tp), tm=256, bm=256)
    cfg_s = dict(bb=bs, tt=ts, tc=ts, ch=min(GCHUNK, ts), tm=ts, bm=128)
    y_p, p_gc, p_g, p_sc, p_rc, p_r = _trunk(
        x_prompt, mods_p, z(DEPTH, bp, GCONV - 1, GQKV), z(DEPTH, bp, GH, GDK, GDV), z(DEPTH, bp, SCK - 1, SCW),
        z(DEPTH, bp, RGK - 1, RGW), z(DEPTH, bp, RGW), layers, fg, cfg=cfg_p)
    y_s, s_gc, s_g, s_sc, s_rc, s_r = _trunk(
        x_sample, mods_s, state_gdn_conv, state_gdn, state_sconv, state_rglru_conv, state_rglru, layers, fg,
        cfg=cfg_s)
    return (y_p, y_s, p_gc, p_g, p_sc, p_rc, p_r, s_gc, s_g, s_sc, s_rc, s_r)
```

```python
import functools
import math

import jax
import jax.numpy as jnp
from jax import lax
from jax.experimental import pallas as pl
from jax.experimental.pallas import tpu as pltpu

F32 = jnp.float32
BF16 = jnp.bfloat16

D = 1024
DEPTH = 4
EPS = 1e-6
N_MOD = 6
GH = 4
GDK = 128
GDV = 128
GQK = GH * GDK
GV = GH * GDV
GQKV = 2 * GQK + GV
GCONV = 4
GCHUNK = 64
SCW = 512
SCK = 3
RGW = 512
RGK = 4
RG_BLOCKS = 8
RG_C = 8.0
NE = 32
TOPK = 4
DE = 1024
SW_ALPHA = 1.702
SW_LIMIT = 7.0

LANES = 128
SUBLANES = 8
VMEM_LIMIT = 56 * 1024 * 1024

NMAIN = 7680
COL_MA, COL_MB, COL_MC = 2, 3, 4
COL_BG, COL_CG, COL_XS, COL_XR, COL_YR = 10, 11, 12, 13, 14


def _cparams(sem):
    return pltpu.CompilerParams(dimension_semantics=sem, vmem_limit_bytes=VMEM_LIMIT)


def _dot(a, b):
    return jnp.dot(a.astype(BF16), b.astype(BF16), preferred_element_type=F32)


def _dot_nt(a, b):
    return lax.dot_general(a.astype(BF16), b.astype(BF16), (((1,), (1,)), ((), ())),
                           preferred_element_type=F32)


def _dot_tn(a, b):
    return lax.dot_general(a.astype(BF16), b.astype(BF16), (((0,), (0,)), ((), ())),
                           preferred_element_type=F32)


def _split3(x):
    x1 = x.astype(BF16)
    r1 = x - x1.astype(F32)
    x2 = r1.astype(BF16)
    r2 = r1 - x2.astype(F32)
    x3 = r2.astype(BF16)
    return x1, x2, x3


def _softplus(x):
    return jnp.maximum(x, 0.0) + jnp.log1p(jnp.exp(-jnp.abs(x)))


def _expm1(x):
    u = jnp.exp(x)
    um1 = u - 1.0
    y = um1 * x / jnp.log(u)
    y = jnp.where(u == 1.0, x, y)
    return jnp.where(um1 == -1.0, -1.0, y)


def _sigmoid(x):
    return 1.0 / (1.0 + jnp.exp(-x))


def _silu(x):
    return x * _sigmoid(x)


def _ada_kernel(c_ref, w_ref, b_ref, o_ref):
    c = c_ref[...]
    o_ref[...] = _dot(_silu(c), w_ref[...]) + b_ref[...]


def _ada_mods(c_all, w_ada, b_ada):
    r = c_all.shape[0]
    tn = 1536
    return pl.pallas_call(
        _ada_kernel,
        grid=(DEPTH, N_MOD * D // tn),
        in_specs=[
            pl.BlockSpec((r, D), lambda l, j: (0, 0)),
            pl.BlockSpec((None, D, tn), lambda l, j: (l, 0, j)),
            pl.BlockSpec((None, 1, tn), lambda l, j: (l, 0, j)),
        ],
        out_specs=pl.BlockSpec((None, r, tn), lambda l, j: (l, 0, j)),
        out_shape=jax.ShapeDtypeStruct((DEPTH, r, N_MOD * D), F32),
        compiler_params=_cparams(("arbitrary", "arbitrary")),
        name="ada_mods",
    )(c_all, w_ada, b_ada.reshape(DEPTH, 1, N_MOD * D))


def _inproj_kernel(x_ref, mod_ref, g_ref, w_ref, wba_ref, proj_ref, ba_ref, h_ref, *, bb, tt):
    j = pl.program_id(2)

    @pl.when(j == 0)
    def _():
        x = x_ref[...]
        var = jnp.mean(x * x, axis=-1, keepdims=True)
        y = x * lax.rsqrt(var + EPS) * g_ref[...]
        h = y * (1.0 + mod_ref[:, 1:2, :]) + mod_ref[:, 0:1, :]
        h2 = h.reshape(bb * tt, D).astype(BF16)
        h_ref[...] = h2
        ba_ref[...] = jnp.dot(h2, wba_ref[...], preferred_element_type=F32).reshape(bb, tt, LANES)

    proj_ref[...] = jnp.dot(h_ref[...], w_ref[...], preferred_element_type=F32).reshape(bb, tt, -1)


def _in_proj(x, mod, n1g, w_main, w_ba, *, bb, tt, l, tn=1536):
    b, t, _ = x.shape
    return pl.pallas_call(
        functools.partial(_inproj_kernel, bb=bb, tt=tt),
        grid=(b // bb, t // tt, NMAIN // tn),
        in_specs=[
            pl.BlockSpec((bb, tt, D), lambda i, s, j: (i, s, 0)),
            pl.BlockSpec((bb, N_MOD, D), lambda i, s, j: (i, 0, 0)),
            pl.BlockSpec((1, D), lambda i, s, j: (0, 0)),
            pl.BlockSpec((None, D, tn), lambda i, s, j: (l, 0, j)),
            pl.BlockSpec((None, D, LANES), lambda i, s, j: (l, 0, 0)),
        ],
        out_specs=[
            pl.BlockSpec((bb, tt, tn), lambda i, s, j: (i, s, j)),
            pl.BlockSpec((bb, tt, LANES), lambda i, s, j: (i, s, 0)),
        ],
        out_shape=[
            jax.ShapeDtypeStruct((b, t, NMAIN), F32),
            jax.ShapeDtypeStruct((b, t, LANES), F32),
        ],
        scratch_shapes=[pltpu.VMEM((bb * tt, D), BF16)],
        compiler_params=_cparams(("arbitrary", "arbitrary", "arbitrary")),
        name="in_proj",
    )(x, mod, n1g, w_main, w_ba)


def _causal_conv(xx_ref, x, w_ref, state0_ref, staten_ref, first, *, k, tc):
    lo = SUBLANES - (k - 1)

    @pl.when(first)
    def _():
        xx_ref[lo:SUBLANES, :] = state0_ref[...]

    @pl.when(jnp.logical_not(first))
    def _():
        xx_ref[lo:SUBLANES, :] = xx_ref[tc + lo:tc + SUBLANES, :]

    xx_ref[SUBLANES:SUBLANES + tc, :] = x
    y = xx_ref[lo:lo + tc, :] * w_ref[0:1, :]
    for j in range(1, k):
        y = y + xx_ref[lo + j:lo + j + tc, :] * w_ref[j:j + 1, :]
    staten_ref[...] = xx_ref[tc + lo:tc + SUBLANES, :]
    return y


def _gdn_kernel(qkvz_ref, ba_ref, conv0_ref, s0_ref, cw_ref, hp_ref, ng_ref,
                o_ref, convn_ref, sn_ref, xx_ref, s_ref, *, tc, ch):
    first = pl.program_id(1) == 0

    @pl.when(first)
    def _():
        for h in range(GH):
            s_ref[:, h * GDV:(h + 1) * GDV] = s0_ref[h]

    y = _causal_conv(xx_ref, qkvz_ref[:, :GQKV], cw_ref, conv0_ref, convn_ref, first, k=GCONV, tc=tc)
    y = _silu(y)
    ba = ba_ref[...]
    beta_all = _sigmoid(ba)
    g_all = -jnp.exp(hp_ref[0:1, :]) * _softplus(ba + hp_ref[1:2, :])

    r = GH * ch
    lg = int(math.log2(ch))
    row = lax.broadcasted_iota(jnp.int32, (r, r), 0)
    col = lax.broadcasted_iota(jnp.int32, (r, r), 1)
    same = (row >> lg) == (col >> lg)
    tril = jnp.logical_and(same, row >= col)
    strict = jnp.logical_and(same, row > col)
    tril_bf = tril.astype(BF16)
    eye = (row == col).astype(F32)
    lvl_masks = [jnp.logical_and(((row >> m) & 1) == 1, (col >> m) == (row >> m) - 1) for m in range(lg)]
    wrow = lax.broadcasted_iota(jnp.int32, (r, GV), 0)
    wcol = lax.broadcasted_iota(jnp.int32, (r, GV), 1)
    own = (wrow >> lg) == (wcol >> int(math.log2(GDV)))
    e_r = lax.broadcasted_iota(jnp.int32, (SUBLANES, LANES), 0)
    e_c = lax.broadcasted_iota(jnp.int32, (SUBLANES, LANES), 1)
    e0 = jnp.logical_and(e_r == 0, e_c == 0).astype(BF16)
    nt = (((1,), (1,)), ((), ()))
    scale = GDK ** -0.5
    ng = ng_ref[...]

    def stack(x, width):
        return jnp.concatenate([x[:, h * width:(h + 1) * width] for h in range(GH)], axis=0)

    for c in range(tc // ch):
        r0, r1 = c * ch, (c + 1) * ch
        q_st = stack(y[r0:r1, 0:GQK], GDK)
        k_st = stack(y[r0:r1, GQK:2 * GQK], GDK)
        v_nat = y[r0:r1, 2 * GQK:]
        q_st = q_st * (lax.rsqrt(jnp.sum(q_st * q_st, axis=-1, keepdims=True) + EPS) * scale)
        k_st = k_st * lax.rsqrt(jnp.sum(k_st * k_st, axis=-1, keepdims=True) + EPS)
        beta_st = jnp.concatenate([beta_all[r0:r1, h:h + 1] for h in range(GH)], axis=0)
        g_st = jnp.concatenate([g_all[r0:r1, GH + h:GH + h + 1] for h in range(GH)], axis=0)
        g1, g2, g3 = _split3(jnp.broadcast_to(g_st, (r, LANES)))
        gc_full = (jnp.dot(tril_bf, g1, preferred_element_type=F32)
                   + jnp.dot(tril_bf, g2, preferred_element_type=F32)
                   + jnp.dot(tril_bf, g3, preferred_element_type=F32))
        gc = gc_full[:, 0:1]
        c1, c2, c3 = _split3(gc_full)
        gc_row = (lax.dot_general(e0, c1, nt, preferred_element_type=F32)
                  + lax.dot_general(e0, c2, nt, preferred_element_type=F32)
                  + lax.dot_general(e0, c3, nt, preferred_element_type=F32))[0:1, :]
        decay = jnp.where(tril, jnp.exp(jnp.where(tril, gc - gc_row, 0.0)), 0.0)
        kb = k_st * beta_st
        a_mat = jnp.where(strict, _dot_nt(kb, k_st) * decay, 0.0)
        dinv = eye - jnp.where(lvl_masks[0], a_mat, 0.0)
        for m in range(1, lg):
            off = jnp.where(lvl_masks[m], a_mat, 0.0)
            dinv = dinv - _dot(_dot(dinv, off), dinv)
        egc = jnp.exp(gc)
        v_bd = jnp.where(own, jnp.concatenate([v_nat] * GH, axis=0), 0.0)
        sol = _dot(dinv, jnp.concatenate([v_bd * beta_st, kb * egc], axis=-1))
        u_bd = sol[:, :GV]
        w_st = sol[:, GV:]
        qk = jnp.where(tril, _dot_nt(q_st, k_st) * decay, 0.0)
        s = s_ref[...]
        v_new = u_bd - jnp.where(own, _dot(w_st, s), 0.0)
        o_bd = jnp.where(own, _dot(q_st * egc, s), 0.0) + _dot(qk, v_new)
        last_rows = [gc_full[h * ch + ch - 1:h * ch + ch, :] for h in range(GH)]
        g_last_st = jnp.concatenate([jnp.broadcast_to(lr[:, 0:1], (ch, 1)) for lr in last_rows], axis=0)
        g_last_wide = jnp.concatenate(last_rows, axis=1)
        s_ref[...] = s * jnp.exp(g_last_wide) + _dot_tn(k_st * jnp.exp(g_last_st - gc), v_new)
        o_nat = o_bd[0:ch]
        for h in range(1, GH):
            o_nat = o_nat + o_bd[h * ch:(h + 1) * ch]
        for h in range(GH):
            o = o_nat[:, h * GDV:(h + 1) * GDV]
            z = qkvz_ref[r0:r1, GQKV + h * GDV:GQKV + (h + 1) * GDV]
            on = o * lax.rsqrt(jnp.mean(o * o, axis=-1, keepdims=True) + EPS) * ng
            o_ref[r0:r1, h * GDV:(h + 1) * GDV] = on * _silu(z)

    for h in range(GH):
        sn_ref[h] = s_ref[:, h * GDV:(h + 1) * GDV]


def _gdn(proj, ba, conv0, s0, conv_w, hp, norm_g, *, tc, ch):
    b, t, _ = proj.shape
    return pl.pallas_call(
        functools.partial(_gdn_kernel, tc=tc, ch=ch),
        grid=(b, t // tc),
        in_specs=[
            pl.BlockSpec((None, tc, GQKV + GV), lambda i, s: (i, s, 0)),
            pl.BlockSpec((None, tc, LANES), lambda i, s: (i, s, 0)),
            pl.BlockSpec((None, GCONV - 1, GQKV), lambda i, s: (i, 0, 0)),
            pl.BlockSpec((None, GH, GDK, GDV), lambda i, s: (i, 0, 0, 0)),
            pl.BlockSpec((GCONV, GQKV), lambda i, s: (0, 0)),
            pl.BlockSpec((2, LANES), lambda i, s: (0, 0)),
            pl.BlockSpec((1, GDV), lambda i, s: (0, 0)),
        ],
        out_specs=[
            pl.BlockSpec((None, tc, GV), lambda i, s: (i, s, 0)),
            pl.BlockSpec((None, GCONV - 1, GQKV), lambda i, s: (i, 0, 0)),
            pl.BlockSpec((None, GH, GDK, GDV), lambda i, s: (i, 0, 0, 0)),
        ],
        out_shape=[
            jax.ShapeDtypeStruct((b, t, GV), F32),
            jax.ShapeDtypeStruct((b, GCONV - 1, GQKV), F32),
            jax.ShapeDtypeStruct((b, GH, GDK, GDV), F32),
        ],
        scratch_shapes=[pltpu.VMEM((tc + SUBLANES, GQKV), F32), pltpu.VMEM((GDK, GV), F32)],
        compiler_params=_cparams(("arbitrary", "arbitrary")),
        name="gdn",
    )(proj, ba, conv0, s0, conv_w, hp, norm_g)


def _convrg_kernel(bg_ref, cg_ref, xs_ref, xr_ref, yr_ref, sc0_ref, rc0_ref, h0_ref,
                   scw_ref, rcw_ref, rcb_ref, wr_ref, br_ref, wi_ref, bi_ref, lam_ref,
                   ob_ref, oc_ref, scn_ref, rcn_ref, hn_ref, xs_scr, xr_scr, h_scr, *, tc):
    first = pl.program_id(1) == 0

    @pl.when(first)
    def _():
        h_scr[...] = jnp.broadcast_to(h0_ref[...], h_scr.shape)

    u = cg_ref[...] * xs_ref[...]
    yb = _causal_conv(xs_scr, u, scw_ref, sc0_ref, scn_ref, first, k=SCK, tc=tc)
    ob_ref[...] = bg_ref[...] * yb

    xc = _causal_conv(xr_scr, xr_ref[...], rcw_ref, rc0_ref, rcn_ref, first, k=RGK, tc=tc) + rcb_ref[...]
    r = _sigmoid(_dot(xc, wr_ref[...]) + br_ref[...])
    i = _sigmoid(_dot(xc, wi_ref[...]) + bi_ref[...])
    log_a = -RG_C * r * _softplus(-lam_ref[...])
    a = jnp.exp(log_a)
    bt = jnp.sqrt(-_expm1(2.0 * log_a)) * (i * xc)
    rows = lax.broadcasted_iota(jnp.int32, (tc, RGW), 0)
    s = 1
    while s < tc:
        keep = rows >= s
        a_sh = jnp.where(keep, pltpu.roll(a, s, 0), 1.0)
        b_sh = jnp.where(keep, pltpu.roll(bt, s, 0), 0.0)
        bt = a * b_sh + bt
        a = a * a_sh
        s *= 2
    h = a * h_scr[0:1, :] + bt
    h_last = h[tc - 1:tc, :]
    h_scr[...] = jnp.broadcast_to(h_last, h_scr.shape)
    hn_ref[...] = h_last
    yr = yr_ref[...]
    gelu = 0.5 * yr * (1.0 + jnp.tanh(math.sqrt(2.0 / math.pi) * (yr + 0.044715 * (yr * yr * yr))))
    oc_ref[...] = h * gelu


def _convrg(proj, sc0, rc0, h0, scw, rcw, rcb, wr, br, wi, bi, lam, *, tc):
    b, t, _ = proj.shape

    def col(cidx):
        return pl.BlockSpec((None, tc, SCW), lambda i, s, cidx=cidx: (i, s, cidx))

    def full2(shape):
        return pl.BlockSpec(shape, lambda i, s: (0, 0))

    return pl.pallas_call(
        functools.partial(_convrg_kernel, tc=tc),
        grid=(b, t // tc),
        in_specs=[
            col(COL_BG), col(COL_CG), col(COL_XS), col(COL_XR), col(COL_YR),
            pl.BlockSpec((None, SCK - 1, SCW), lambda i, s: (i, 0, 0)),
            pl.BlockSpec((None, RGK - 1, RGW), lambda i, s: (i, 0, 0)),
            pl.BlockSpec((None, 1, RGW), lambda i, s: (i, 0, 0)),
            full2((SCK, SCW)), full2((RGK, RGW)), full2((1, RGW)),
            full2((RGW, RGW)), full2((1, RGW)), full2((RGW, RGW)), full2((1, RGW)), full2((1, RGW)),
        ],
        out_specs=[
            pl.BlockSpec((None, tc, SCW), lambda i, s: (i, s, 0)),
            pl.BlockSpec((None, tc, RGW), lambda i, s: (i, s, 0)),
            pl.BlockSpec((None, SCK - 1, SCW), lambda i, s: (i, 0, 0)),
            pl.BlockSpec((None, RGK - 1, RGW), lambda i, s: (i, 0, 0)),
            pl.BlockSpec((None, 1, RGW), lambda i, s: (i, 0, 0)),
        ],
        out_shape=[
            jax.ShapeDtypeStruct((b, t, SCW), F32),
            jax.ShapeDtypeStruct((b, t, RGW), F32),
            jax.ShapeDtypeStruct((b, SCK - 1, SCW), F32),
            jax.ShapeDtypeStruct((b, RGK - 1, RGW), F32),
            jax.ShapeDtypeStruct((b, 1, RGW), F32),
        ],
        scratch_shapes=[pltpu.VMEM((tc + SUBLANES, SCW), F32), pltpu.VMEM((tc + SUBLANES, RGW), F32),
                        pltpu.VMEM((SUBLANES, RGW), F32)],
        compiler_params=_cparams(("arbitrary", "arbitrary")),
        name="convrg",
    )(proj, proj, proj, proj, proj, sc0, rc0, h0, scw, rcw, rcb, wr, br, wi, bi, lam)


def _merge_kernel(x_ref, oa_ref, ob_ref, oc_ref, ma_ref, mb_ref, mc_ref, mod_ref,
                  wa_ref, wb_ref, wc_ref, wo_ref, g2_ref, rw_ref, rb_ref,
                  x1_ref, h2_ref, lg_ref, *, bb, tt):
    n = bb * tt

    def flat(ref):
        return ref[...].reshape(n, ref.shape[-1])

    merged = (_sigmoid(flat(ma_ref)) * _dot(flat(oa_ref), wa_ref[...])
              + _sigmoid(flat(mb_ref)) * _dot(flat(ob_ref), wb_ref[...])
              + _sigmoid(flat(mc_ref)) * _dot(flat(oc_ref), wc_ref[...]))
    y = _dot(merged, wo_ref[...]).reshape(bb, tt, D)
    x1 = x_ref[...] + mod_ref[:, 2:3, :] * y
    x1_ref[...] = x1
    var = jnp.mean(x1 * x1, axis=-1, keepdims=True)
    h2 = (x1 * lax.rsqrt(var + EPS) * g2_ref[...]) * (1.0 + mod_ref[:, 4:5, :]) + mod_ref[:, 3:4, :]
    h2_ref[...] = h2
    hf = h2.reshape(n, D)
    h_hi = hf.astype(BF16)
    h_lo = (hf - h_hi.astype(F32)).astype(BF16)
    rw = rw_ref[...]
    r_hi = rw.astype(BF16)
    r_lo = (rw - r_hi.astype(F32)).astype(BF16)
    lg = (jnp.dot(h_hi, r_hi, preferred_element_type=F32)
          + (jnp.dot(h_hi, r_lo, preferred_element_type=F32) + jnp.dot(h_lo, r_hi, preferred_element_type=F32)))
    lg_ref[...] = (lg + rb_ref[...]).reshape(bb, tt, LANES)


def _merge(x, oa, ob, oc, proj, mod, wa, wb, wc, wo, n2g, rw, rb, *, bb, tt):
    b, t, _ = x.shape

    def tok(width, cidx=0):
        return pl.BlockSpec((bb, tt, width), lambda i, s, cidx=cidx: (i, s, cidx))

    def full2(shape):
        return pl.BlockSpec(shape, lambda i, s: (0, 0))

    return pl.pallas_call(
        functools.partial(_merge_kernel, bb=bb, tt=tt),
        grid=(b // bb, t // tt),
        in_specs=[
            tok(D), tok(GV), tok(SCW), tok(RGW), tok(D, COL_MA), tok(D, COL_MB), tok(D, COL_MC),
            pl.BlockSpec((bb, N_MOD, D), lambda i, s: (i, 0, 0)),
            full2((GV, D)), full2((SCW, D)), full2((RGW, D)), full2((D, D)), full2((1, D)),
            full2((D, LANES)), full2((1, LANES)),
        ],
        out_specs=[tok(D), tok(D), tok(LANES)],
        out_shape=[
            jax.ShapeDtypeStruct((b, t, D), F32),
            jax.ShapeDtypeStruct((b, t, D), F32),
            jax.ShapeDtypeStruct((b, t, LANES), F32),
        ],
        compiler_params=_cparams(("arbitrary", "arbitrary")),
        name="merge",
    )(x, oa, ob, oc, proj, proj, proj, mod, wa, wb, wc, wo, n2g, rw, rb)


ROUTE_E, ROUTE_RANK, ROUTE_GATE = 0, 4, 8


def _route_kernel(lg_ref, rec_ref, cnt_ref, run_ref, *, tm):
    @pl.when(pl.program_id(0) == 0)
    def _():
        run_ref[...] = jnp.zeros_like(run_ref)

    lane = lax.broadcasted_iota(jnp.int32, (tm, LANES), 1)
    lanef = lane.astype(F32)
    neg = jnp.float32(-jnp.inf)
    cur = jnp.where(lane < NE, lg_ref[...], neg)
    vals, hots = [], []
    for _ in range(TOPK):
        m = jnp.max(cur, axis=-1, keepdims=True)
        idx = jnp.min(jnp.where(cur == m, lanef, float(LANES)), axis=-1, keepdims=True)
        hot = lanef == idx
        cur = jnp.where(hot, neg, cur)
        vals.append(m)
        hots.append(hot)
    es = [jnp.exp(v - vals[0]) for v in vals]
    den = es[0] + es[1] + es[2] + es[3]
    cnt = hots[0].astype(F32) + hots[1].astype(F32) + hots[2].astype(F32) + hots[3].astype(F32)
    r = lax.broadcasted_iota(jnp.int32, (tm, tm), 0)
    c = lax.broadcasted_iota(jnp.int32, (tm, tm), 1)
    prefix = jnp.dot((r > c).astype(BF16), cnt.astype(BF16), preferred_element_type=F32)
    base = prefix + run_ref[0:1, :]
    rec = jnp.zeros((tm, LANES), F32)
    for k in range(TOPK):
        e_k = jnp.sum(jnp.where(hots[k], lanef, 0.0), axis=-1, keepdims=True)
        rank_k = jnp.sum(jnp.where(hots[k], base, 0.0), axis=-1, keepdims=True)
        rec = jnp.where(lane == ROUTE_E + k, e_k, rec)
        rec = jnp.where(lane == ROUTE_RANK + k, rank_k, rec)
        rec = jnp.where(lane == ROUTE_GATE + k, es[k] / den, rec)
    rec_ref[...] = rec
    run = run_ref[0:1, :] + jnp.sum(cnt, axis=0, keepdims=True)
    run_ref[...] = jnp.broadcast_to(run, run_ref.shape)
    cnt_ref[...] = jnp.broadcast_to(run, cnt_ref.shape)


def _route(logits, *, tm):
    n = logits.shape[0]
    return pl.pallas_call(
        functools.partial(_route_kernel, tm=tm),
        grid=(n // tm,),
        in_specs=[pl.BlockSpec((tm, LANES), lambda i: (i, 0))],
        out_specs=[pl.BlockSpec((tm, LANES), lambda i: (i, 0)), pl.BlockSpec((SUBLANES, LANES), lambda i: (0, 0))],
        out_shape=[jax.ShapeDtypeStruct((n, LANES), F32), jax.ShapeDtypeStruct((SUBLANES, LANES), F32)],
        scratch_shapes=[pltpu.VMEM((SUBLANES, LANES), F32)],
        compiler_params=_cparams(("arbitrary",)),
        name="route",
    )(logits)


ROW_WAIT_UNROLL = 16


def _wait_rows(n, any_row_copy):
    def wait(_, carry):
        for _ in range(ROW_WAIT_UNROLL):
            any_row_copy.wait()
        return carry

    lax.fori_loop(0, n // ROW_WAIT_UNROLL, wait, 0)


def _dispatch_kernel(dest_ref, h2_ref, zin_ref, xr_ref, sem, *, tm):
    del zin_ref

    def start(j, carry):
        for k in range(TOPK):
            pltpu.make_async_copy(h2_ref.at[pl.ds(j, 1)], xr_ref.at[pl.ds(dest_ref[0, j * TOPK + k], 1)], sem).start()
        return carry

    lax.fori_loop(0, tm, start, 0, unroll=4)
    _wait_rows(tm * TOPK, pltpu.make_async_copy(h2_ref.at[pl.ds(0, 1)], xr_ref.at[pl.ds(0, 1)], sem))


def _dispatch(dest_tiles, h2, n_rows, *, tm):
    n = h2.shape[0]
    zeros = jnp.zeros((n_rows, D), F32)
    return pl.pallas_call(
        functools.partial(_dispatch_kernel, tm=tm),
        grid=(n // tm,),
        in_specs=[
            pl.BlockSpec((None, 1, tm * TOPK), lambda i: (i, 0, 0), memory_space=pltpu.SMEM),
            pl.BlockSpec((tm, D), lambda i: (i, 0)),
            pl.BlockSpec(memory_space=pl.ANY),
        ],
        out_specs=pl.BlockSpec(memory_space=pl.ANY),
        out_shape=jax.ShapeDtypeStruct((n_rows, D), F32),
        scratch_shapes=[pltpu.SemaphoreType.DMA],
        input_output_aliases={2: 0},
        compiler_params=_cparams(("arbitrary",)),
        name="dispatch",
    )(dest_tiles, h2, zeros)


def _expert_kernel(be_ref, x_ref, wgu_ref, bgu_ref, wd_ref, bd_ref, y_ref, wgu_bf, wd_bf):
    i = pl.program_id(0)
    prev = be_ref[jnp.maximum(i - 1, 0)]
    changed = jnp.logical_or(i == 0, be_ref[i] != prev)

    @pl.when(changed)
    def _():
        wgu_bf[...] = wgu_ref[...].astype(BF16)
        wd_bf[...] = wd_ref[...].astype(BF16)

    x = x_ref[...].astype(BF16)
    gu = jnp.dot(x, wgu_bf[...], preferred_element_type=F32) + bgu_ref[...]
    gate = jnp.minimum(gu[:, :DE], SW_LIMIT)
    up = jnp.clip(gu[:, DE:], -SW_LIMIT, SW_LIMIT)
    act = (up + 1.0) * gate * _sigmoid(gate * SW_ALPHA)
    y_ref[...] = jnp.dot(act.astype(BF16), wd_bf[...], preferred_element_type=F32) + bd_ref[...]


def _experts(block_e, x_rows, wgu, bgu, wd, bd, *, bm, l):
    n_rows = x_rows.shape[0]
    grid_spec = pltpu.PrefetchScalarGridSpec(
        num_scalar_prefetch=1,
        grid=(n_rows // bm,),
        in_specs=[
            pl.BlockSpec((bm, D), lambda i, be: (i, 0)),
            pl.BlockSpec((None, None, D, 2 * DE), lambda i, be: (l, be[i], 0, 0)),
            pl.BlockSpec((None, None, 1, 2 * DE), lambda i, be: (l, be[i], 0, 0)),
            pl.BlockSpec((None, None, DE, D), lambda i, be: (l, be[i], 0, 0)),
            pl.BlockSpec((None, None, 1, D), lambda i, be: (l, be[i], 0, 0)),
        ],
        out_specs=pl.BlockSpec((bm, D), lambda i, be: (i, 0)),
        scratch_shapes=[pltpu.VMEM((D, 2 * DE), BF16), pltpu.VMEM((DE, D), BF16)],
    )
    return pl.pallas_call(
        _expert_kernel,
        grid_spec=grid_spec,
        out_shape=jax.ShapeDtypeStruct((n_rows, D), F32),
        compiler_params=_cparams(("arbitrary",)),
        name="experts",
    )(block_e, x_rows, wgu, bgu, wd, bd)


def _combine_kernel(dest_ref, yr_ref, rec_ref, x1_ref, mod_ref, fg_ref, x2_ref, buf, sem, *, tm, final):
    def start(j, carry):
        for k in range(TOPK):
            pltpu.make_async_copy(yr_ref.at[pl.ds(dest_ref[0, j * TOPK + k], 1)],
                                  buf.at[pl.ds(k * tm + j, 1)], sem).start()
        return carry

    lax.fori_loop(0, tm, start, 0, unroll=4)
    _wait_rows(tm * TOPK, pltpu.make_async_copy(yr_ref.at[pl.ds(0, 1)], buf.at[pl.ds(0, 1)], sem))
    rec = rec_ref[...]
    ff = rec[:, ROUTE_GATE:ROUTE_GATE + 1] * buf[0:tm, :]
    for k in range(1, TOPK):
        ff = ff + rec[:, ROUTE_GATE + k:ROUTE_GATE + k + 1] * buf[k * tm:(k + 1) * tm, :]
    x2 = x1_ref[...] + mod_ref[...] * ff
    if final:
        var = jnp.mean(x2 * x2, axis=-1, keepdims=True)
        x2 = x2 * lax.rsqrt(var + EPS) * fg_ref[...]
    x2_ref[...] = x2


def _combine(dest_tiles, y_rows, rec, x1, g2_rows, final_g, *, tm, rows_per_seq, final):
    n = x1.shape[0]
    return pl.pallas_call(
        functools.partial(_combine_kernel, tm=tm, final=final),
        grid=(n // tm,),
        in_specs=[
            pl.BlockSpec((None, 1, tm * TOPK), lambda i: (i, 0, 0), memory_space=pltpu.SMEM),
            pl.BlockSpec(memory_space=pl.ANY),
            pl.BlockSpec((tm, LANES), lambda i: (i, 0)),
            pl.BlockSpec((tm, D), lambda i: (i, 0)),
            pl.BlockSpec((None, 1, D), lambda i: ((i * tm) // rows_per_seq, 0, 0)),
            pl.BlockSpec((1, D), lambda i: (0, 0)),
        ],
        out_specs=pl.BlockSpec((tm, D), lambda i: (i, 0)),
        out_shape=jax.ShapeDtypeStruct((n, D), F32),
        scratch_shapes=[pltpu.VMEM((TOPK * tm, D), F32), pltpu.SemaphoreType.DMA],
        compiler_params=_cparams(("arbitrary",)),
        name="combine",
    )(dest_tiles, y_rows, rec, x1, g2_rows, final_g)


def _moe(h2, logits, x1, g2_rows, wgu, bgu, wd, bd, final_g, *, tm, bm, rows_per_seq, l):
    n = h2.shape[0]
    rec, cnt = _route(logits, tm=tm)
    counts = cnt[0, :NE].astype(jnp.int32)
    padded = (counts + bm - 1) // bm * bm
    pad_end = jnp.cumsum(padded)
    pad_start = pad_end - padded
    e_idx = rec[:, ROUTE_E:ROUTE_E + TOPK].astype(jnp.int32)
    rank = rec[:, ROUTE_RANK:ROUTE_RANK + TOPK].astype(jnp.int32)
    dest = pad_start[e_idx] + rank
    n_blocks = -(-(n * TOPK + NE * (bm - 1)) // bm)
    n_rows = n_blocks * bm
    block_start = jnp.arange(n_blocks, dtype=jnp.int32) * bm
    block_e = jnp.minimum(jnp.sum((pad_end[None, :] <= block_start[:, None]).astype(jnp.int32), axis=1), NE - 1)
    dest_tiles = dest.reshape(n // tm, 1, tm * TOPK)
    x_rows = _dispatch(dest_tiles, h2, n_rows, tm=tm)
    y_rows = _experts(block_e, x_rows, wgu, bgu, wd, bd, bm=bm, l=l)
    return _combine(dest_tiles, y_rows, rec, x1, g2_rows, final_g, tm=tm, rows_per_seq=rows_per_seq,
                    final=(l == DEPTH - 1))


def _prep_shared(p):
    w_in = p['w_in']
    o = 0
    parts = {}
    for name, width in (('qkv', GQKV), ('z', GV), ('b', GH), ('a', GH), ('bg', SCW), ('cg', SCW), ('xs', SCW),
                        ('xr', RGW), ('yr', RGW), ('ma', D), ('mb', D), ('mc', D)):
        parts[name] = w_in[:, :, o:o + width]
        o += width
    w_main = jnp.concatenate([parts[k] for k in ('qkv', 'z', 'ma', 'mb', 'mc', 'bg', 'cg', 'xs', 'xr', 'yr')],
                             axis=2).astype(BF16)
    w_ba = jnp.concatenate([parts['b'], parts['a'], jnp.zeros((DEPTH, D, LANES - 2 * GH), F32)],
                           axis=2).astype(BF16)
    return dict(w_main=w_main, w_ba=w_ba, wgu=p['moe_w_gu'], bgu=p['moe_b_gu'].reshape(DEPTH, NE, 1, 2 * DE),
                wd=p['moe_w_down'], bd=p['moe_b_down'].reshape(DEPTH, NE, 1, D))


def _prep_layer(p, l):
    hp = jnp.zeros((2, LANES), F32)
    hp = hp.at[0, GH:2 * GH].set(p['gdn_a_log'][l]).at[1, GH:2 * GH].set(p['gdn_dt_bias'][l])
    eye = jnp.eye(RG_BLOCKS, dtype=F32)
    wr = jnp.einsum('hij,hk->hikj', p['rg_w_r'][l], eye).reshape(RGW, RGW).astype(BF16)
    wi = jnp.einsum('hij,hk->hikj', p['rg_w_i'][l], eye).reshape(RGW, RGW).astype(BF16)
    rw = jnp.concatenate([p['router_w'][l], jnp.zeros((D, LANES - NE), F32)], axis=1)
    rb = jnp.concatenate([p['router_b'][l], jnp.zeros((LANES - NE,), F32)]).reshape(1, LANES)
    return dict(
        hp=hp, wr=wr, wi=wi, rw=rw, rb=rb,
        n1g=p['norm1_g'][l].reshape(1, D), n2g=p['norm2_g'][l].reshape(1, D),
        gdn_cw=p['gdn_conv_w'][l], gdn_ng=p['gdn_norm_g'][l].reshape(1, GDV),
        scw=p['sconv_w'][l], rcw=p['rg_conv_w'][l], rcb=p['rg_conv_b'][l].reshape(1, RGW),
        br=p['rg_b_r'][l].reshape(1, RGW), bi=p['rg_b_i'][l].reshape(1, RGW), lam=p['rg_lambda'][l].reshape(1, RGW),
        wa=p['w_out_a'][l].astype(BF16), wb=p['w_out_b'][l].astype(BF16), wc=p['w_out_c'][l].astype(BF16),
        wo=p['w_o'][l].astype(BF16),
    )


def _trunk(x, mods, gdn_conv0, gdn0, sconv0, rg_conv0, rg0, layers, shared, final_g, *, cfg):
    b, t, _ = x.shape
    n = b * t
    n_gc, n_g, n_sc, n_rc, n_r = [], [], [], [], []
    for l in range(DEPTH):
        lw = layers[l]
        mod = mods[l]
        proj, ba = _in_proj(x, mod, lw['n1g'], shared['w_main'], shared['w_ba'], bb=cfg['bb'], tt=cfg['tt'], l=l)
        o_a, gc_new, g_new = _gdn(proj, ba, gdn_conv0[l], gdn0[l], lw['gdn_cw'], lw['hp'], lw['gdn_ng'],
                                  tc=cfg['tc'], ch=cfg['ch'])
        o_b, o_c, sc_new, rc_new, r_new = _convrg(
            proj, sconv0[l], rg_conv0[l], rg0[l].reshape(b, 1, RGW), lw['scw'], lw['rcw'], lw['rcb'],
            lw['wr'], lw['br'], lw['wi'], lw['bi'], lw['lam'], tc=cfg['tc'])
        x1, h2, logits = _merge(x, o_a, o_b, o_c, proj, mod, lw['wa'], lw['wb'], lw['wc'], lw['wo'], lw['n2g'],
                                lw['rw'], lw['rb'], bb=cfg['bb'], tt=cfg['tt'])
        g2_rows = mod[:, 5:6, :]
        x2 = _moe(h2.reshape(n, D), logits.reshape(n, LANES), x1.reshape(n, D), g2_rows,
                  shared['wgu'], shared['bgu'], shared['wd'], shared['bd'], final_g,
                  tm=cfg['tm'], bm=cfg['bm'], rows_per_seq=t, l=l)
        x = x2.reshape(b, t, D)
        n_gc.append(gc_new)
        n_g.append(g_new)
        n_sc.append(sc_new)
        n_rc.append(rc_new)
        n_r.append(r_new.reshape(b, RGW))
    return x, jnp.stack(n_gc), jnp.stack(n_g), jnp.stack(n_sc), jnp.stack(n_rc), jnp.stack(n_r)


def kernel(x_prompt, x_sample, c_prompt, c_sample, state_gdn_conv, state_gdn, state_sconv, state_rglru_conv, state_rglru, norm1_g, norm2_g, final_g, w_ada, b_ada, w_in, gdn_conv_w, gdn_a_log, gdn_dt_bias, gdn_norm_g, w_out_a, sconv_w, w_out_b, rg_conv_w, rg_conv_b, rg_w_r, rg_b_r, rg_w_i, rg_b_i, rg_lambda, w_out_c, w_o, router_w, router_b, moe_w_gu, moe_b_gu, moe_w_down, moe_b_down):
    p = dict(norm1_g=norm1_g, norm2_g=norm2_g, w_in=w_in, gdn_conv_w=gdn_conv_w, gdn_a_log=gdn_a_log,
             gdn_dt_bias=gdn_dt_bias, gdn_norm_g=gdn_norm_g, w_out_a=w_out_a, sconv_w=sconv_w, w_out_b=w_out_b,
             rg_conv_w=rg_conv_w, rg_conv_b=rg_conv_b, rg_w_r=rg_w_r, rg_b_r=rg_b_r, rg_w_i=rg_w_i, rg_b_i=rg_b_i,
             rg_lambda=rg_lambda, w_out_c=w_out_c, w_o=w_o, router_w=router_w, router_b=router_b,
             moe_w_gu=moe_w_gu, moe_b_gu=moe_b_gu, moe_w_down=moe_w_down, moe_b_down=moe_b_down)
    bp, tp, _ = x_prompt.shape
    bs, ts, _ = x_sample.shape
    layers = [_prep_layer(p, l) for l in range(DEPTH)]
    shared = _prep_shared(p)
    r_pad = -(-(bp + bs) // SUBLANES) * SUBLANES
    c_all = jnp.concatenate([c_prompt, c_sample, jnp.zeros((r_pad - bp - bs, D), F32)], axis=0)
    mods = _ada_mods(c_all, w_ada, b_ada)
    mods_p = [mods[l, :bp].reshape(bp, N_MOD, D) for l in range(DEPTH)]
    mods_s = [mods[l, bp:bp + bs].reshape(bs, N_MOD, D) for l in range(DEPTH)]
    fg = final_g.reshape(1, D)
    f32 = x_prompt.dtype
    z = lambda *s: jnp.zeros(s, f32)
    cfg_p = dict(bb=1, tt=512, tc=256, ch=min(GCHUNK, tp), tm=256, bm=256)
    cfg_s = dict(bb=bs, tt=ts, tc=ts, ch=min(GCHUNK, ts), tm=ts, bm=128)
    y_p, p_gc, p_g, p_sc, p_rc, p_r = _trunk(
        x_prompt, mods_p, z(DEPTH, bp, GCONV - 1, GQKV), z(DEPTH, bp, GH, GDK, GDV), z(DEPTH, bp, SCK - 1, SCW),
        z(DEPTH, bp, RGK - 1, RGW), z(DEPTH, bp, RGW), layers, shared, fg, cfg=cfg_p)
    y_s, s_gc, s_g, s_sc, s_rc, s_r = _trunk(
        x_sample, mods_s, state_gdn_conv, state_gdn, state_sconv, state_rglru_conv, state_rglru, layers, shared, fg,
        cfg=cfg_s)
    return (y_p, y_s, p_gc, p_g, p_sc, p_rc, p_r, s_gc, s_g, s_sc, s_rc, s_r)
```
